```python
import jax, jax.numpy as jnp
from jax import lax
import numpy as np

D_MODEL = 1024
BATCH = 16
SEQ = 2048
DEPTH = 2

N_MEM = 256
EPS = 1e-6

SSD_EXPAND = 2
SSD_D_INNER = SSD_EXPAND * D_MODEL
SSD_HEAD_DIM = 64
SSD_HEADS = SSD_D_INNER // SSD_HEAD_DIM
SSD_GROUPS = 8
SSD_STATE = 128
SSD_CONV = 4
SSD_CHUNK = 128
SSD_XBC = SSD_D_INNER + 2 * SSD_GROUPS * SSD_STATE

CONV_CH = D_MODEL
CONV_WIDTH = 31

CA_HEADS = 4
CA_HEAD_DIM = D_MODEL // CA_HEADS
CA_WIDTH = CA_HEADS * CA_HEAD_DIM

PEER_HEADS = 8
PEER_NKEYS = 128
PEER_EXPERTS = PEER_NKEYS * PEER_NKEYS
PEER_TOPK = 16
PEER_QDIM = 256
PEER_HALF = PEER_QDIM // 2
PEER_TOKEN_BLOCK = 128

N_SSD_LAYERS = (DEPTH + 1) // 2
N_CONV_LAYERS = DEPTH // 2

SSD_IN = SSD_D_INNER + SSD_XBC + SSD_HEADS + CA_WIDTH
CONV_IN = 2 * CONV_CH + CA_WIDTH
SSD_MIX = SSD_D_INNER + CA_WIDTH
CONV_MIX = CONV_CH + CA_WIDTH

kernel_name = "hybrid_ssd_conformer_peer_memxattn"

F32 = jnp.float32


def rms_norm(x, g):
    xf = x.astype(F32)
    y = xf * lax.rsqrt(jnp.mean(xf * xf, axis=-1, keepdims=True) + EPS)
    return (y * g.astype(F32)).astype(x.dtype)


def layer_norm(x, g, b):
    xf = x.astype(F32)
    mu = jnp.mean(xf, axis=-1, keepdims=True)
    var = jnp.mean(jnp.square(xf - mu), axis=-1, keepdims=True)
    return ((xf - mu) * lax.rsqrt(var + EPS) * g.astype(F32) + b.astype(F32)).astype(x.dtype)


def causal_depthwise_conv(x, w, b):
    k = w.shape[0]
    xp = jnp.pad(x, ((0, 0), (k - 1, 0), (0, 0)))
    y = lax.conv_general_dilated(
        xp, w[:, None, :].astype(x.dtype), window_strides=(1,), padding='VALID',
        dimension_numbers=('NWC', 'WIO', 'NWC'), feature_group_count=x.shape[-1])
    return y + b.astype(x.dtype)


def memory_attention(q, mem_n, w_mem_kv):
    b, s, _ = q.shape
    m = mem_n.shape[1]
    k, v = jnp.split(mem_n @ w_mem_kv, 2, axis=-1)
    q = q.reshape(b, s, CA_HEADS, CA_HEAD_DIM)
    k = k.reshape(b, m, CA_HEADS, CA_HEAD_DIM)
    v = v.reshape(b, m, CA_HEADS, CA_HEAD_DIM)
    scores = jnp.einsum('bshd,bmhd->bhsm', q, k, preferred_element_type=F32) * (CA_HEAD_DIM ** -0.5)
    p = jax.nn.softmax(scores, axis=-1).astype(v.dtype)
    o = jnp.einsum('bhsm,bmhd->bshd', p, v)
    return o.reshape(b, s, CA_WIDTH)


def ssd_scan(x, dt, a, bm, cm):
    b, l, h, p = x.shape
    g, n = bm.shape[2], bm.shape[3]
    r = h // g
    q = SSD_CHUNK
    c = l // q
    xd = (x * dt[..., None]).reshape(b, c, q, g, r, p)
    a_cum = jnp.cumsum((dt * a).reshape(b, c, q, g, r), axis=2)
    bc = bm.reshape(b, c, q, g, n)
    cc = cm.reshape(b, c, q, g, n)
    causal = jnp.tril(jnp.ones((q, q), dtype=bool))[:, :, None, None]
    diff = a_cum[:, :, :, None] - a_cum[:, :, None]
    decay = jnp.exp(jnp.where(causal, diff, -jnp.inf))
    cb = jnp.einsum('bctgn,bcsgn->bctsg', cc, bc)
    y_diag = jnp.einsum('bctsgr,bcsgrp->bctgrp', cb[..., None] * decay, xd)
    decay_to_end = jnp.exp(a_cum[:, :, -1:] - a_cum)
    states = jnp.einsum('bcsgn,bcsgrp->bcgrpn', bc, xd * decay_to_end[..., None])
    chunk_decay = jnp.exp(a_cum[:, :, -1])

    def step(carry, inp):
        st, dec = inp
        return carry * dec[..., None, None] + st, carry

    init = jnp.zeros((b, g, r, p, n), dtype=states.dtype)
    _, prev = lax.scan(step, init, (jnp.moveaxis(states, 1, 0), jnp.moveaxis(chunk_decay, 1, 0)))
    prev = jnp.moveaxis(prev, 0, 1)
    y_off = jnp.einsum('bctgn,bcgrpn->bctgrp', cc, prev) * jnp.exp(a_cum)[..., None]
    return (y_diag + y_off).reshape(b, l, h, p)


def ssd_mix(xn, mem_n, w_in, conv_w, conv_b, dt_bias, a_log, d_skip, gate_norm_g, w_mem_kv, w_out):
    b, l, _ = xn.shape
    proj = xn @ w_in
    z, xbc, dt, q = jnp.split(proj, [SSD_D_INNER, SSD_D_INNER + SSD_XBC, SSD_D_INNER + SSD_XBC + SSD_HEADS], axis=-1)
    xbc = jax.nn.silu(causal_depthwise_conv(xbc, conv_w, conv_b))
    xs, bm, cm = jnp.split(xbc, [SSD_D_INNER, SSD_D_INNER + SSD_GROUPS * SSD_STATE], axis=-1)
    dt = jax.nn.softplus(dt.astype(F32) + dt_bias.astype(F32))
    a = -jnp.exp(a_log.astype(F32))
    xh = xs.astype(F32).reshape(b, l, SSD_HEADS, SSD_HEAD_DIM)
    y = ssd_scan(xh, dt, a,
                 bm.astype(F32).reshape(b, l, SSD_GROUPS, SSD_STATE),
                 cm.astype(F32).reshape(b, l, SSD_GROUPS, SSD_STATE))
    y = y + d_skip.astype(F32)[:, None] * xh
    y = y.reshape(b, l, SSD_D_INNER).astype(xn.dtype)
    y = rms_norm(y * jax.nn.silu(z), gate_norm_g)
    o_mem = memory_attention(q, mem_n, w_mem_kv)
    return jnp.concatenate([y, o_mem], axis=-1) @ w_out


def conformer_mix(xn, mem_n, w_in, b_glu, dw_w, dw_b, ln_g, ln_b, w_mem_kv, w_out):
    proj = xn @ w_in
    u, q = jnp.split(proj, [2 * CONV_CH], axis=-1)
    u = jax.nn.glu(u + b_glu, axis=-1)
    u = causal_depthwise_conv(u, dw_w, dw_b)
    u = jax.nn.silu(layer_norm(u, ln_g, ln_b))
    o_mem = memory_attention(q, mem_n, w_mem_kv)
    return jnp.concatenate([u, o_mem], axis=-1) @ w_out


def peer(xn, w_query, sub_keys, expert_u, expert_v):
    b, l, d = xn.shape
    blocks = xn.reshape(-1, PEER_TOKEN_BLOCK, d)

    def block(xt):
        t = xt.shape[0]
        qh = (xt @ w_query).reshape(t, PEER_HEADS, 2, PEER_HALF)
        s = jnp.einsum('thid,hikd->thik', qh, sub_keys, preferred_element_type=F32)
        sv, si = lax.top_k(s, PEER_TOPK)
        cand = (sv[:, :, 0, :, None] + sv[:, :, 1, None, :]).reshape(t, PEER_HEADS, PEER_TOPK * PEER_TOPK)
        cand_idx = (si[:, :, 0, :, None] * PEER_NKEYS + si[:, :, 1, None, :]).reshape(t, PEER_HEADS, PEER_TOPK * PEER_TOPK)
        top, pos = lax.top_k(cand, PEER_TOPK)
        eidx = jnp.take_along_axis(cand_idx, pos, axis=-1)
        gate = jax.nn.softmax(top, axis=-1).astype(xt.dtype)
        u = jnp.take(expert_u, eidx, axis=0)
        v = jnp.take(expert_v, eidx, axis=0)
        hid = jax.nn.gelu(jnp.einsum('thkd,td->thk', u, xt))
        return jnp.einsum('thk,thkd->td', gate * hid, v)

    return lax.map(block, blocks).reshape(b, l, d)


def setup_inputs(seed: int = 0) -> dict:
    key = jax.random.key(seed)
    ks = iter(jax.random.split(key, 40))

    def nrm(shape, scale):
        return jax.random.normal(next(ks), shape, F32) * scale

    def gain(shape):
        return 1.0 + nrm(shape, 0.02)

    ns, nc, D = N_SSD_LAYERS, N_CONV_LAYERS, D_MODEL
    x = nrm((BATCH, SEQ, D), 1.0)
    mem = nrm((BATCH, N_MEM, D), 1.0)
    ssd_norm_g = gain((ns, D))
    ssd_mem_norm_g = gain((ns, D))
    ssd_w_in = nrm((ns, D, SSD_IN), D ** -0.5)
    ssd_conv_w = nrm((ns, SSD_CONV, SSD_XBC), SSD_CONV ** -0.5)
    ssd_conv_b = nrm((ns, SSD_XBC), 0.01)
    dt0 = jnp.exp(jax.random.uniform(next(ks), (ns, SSD_HEADS), F32, np.log(1e-3), np.log(1e-1)))
    ssd_dt_bias = dt0 + jnp.log(-jnp.expm1(-dt0))
    ssd_a_log = jnp.log(jax.random.uniform(next(ks), (ns, SSD_HEADS), F32, 1.0, 16.0))
    ssd_d_skip = gain((ns, SSD_HEADS))
    ssd_gate_norm_g = gain((ns, SSD_D_INNER))
    ssd_w_mem_kv = nrm((ns, D, 2 * CA_WIDTH), D ** -0.5)
    ssd_w_out = nrm((ns, SSD_MIX, D), SSD_MIX ** -0.5)
    cnv_norm_g = gain((nc, D))
    cnv_mem_norm_g = gain((nc, D))
    cnv_w_in = nrm((nc, D, CONV_IN), D ** -0.5)
    cnv_b_glu = nrm((nc, 2 * CONV_CH), 0.01)
    cnv_dw_w = nrm((nc, CONV_WIDTH, CONV_CH), CONV_WIDTH ** -0.5)
    cnv_dw_b = nrm((nc, CONV_CH), 0.01)
    cnv_ln_g = gain((nc, CONV_CH))
    cnv_ln_b = nrm((nc, CONV_CH), 0.01)
    cnv_w_mem_kv = nrm((nc, D, 2 * CA_WIDTH), D ** -0.5)
    cnv_w_out = nrm((nc, CONV_MIX, D), CONV_MIX ** -0.5)
    ffn_norm_g = gain((DEPTH, D))
    peer_w_query = nrm((DEPTH, D, PEER_HEADS * PEER_QDIM), D ** -0.5)
    peer_sub_keys = nrm((DEPTH, PEER_HEADS, 2, PEER_NKEYS, PEER_HALF), PEER_HALF ** -0.5)
    peer_u = nrm((DEPTH, PEER_EXPERTS, D), D ** -0.5)
    peer_v = nrm((DEPTH, PEER_EXPERTS, D), PEER_HEADS ** -0.5)
    final_norm_g = gain((D,))
    return {
        "x": x, "mem": mem,
        "ssd_norm_g": ssd_norm_g, "ssd_mem_norm_g": ssd_mem_norm_g, "ssd_w_in": ssd_w_in,
        "ssd_conv_w": ssd_conv_w, "ssd_conv_b": ssd_conv_b, "ssd_dt_bias": ssd_dt_bias,
        "ssd_a_log": ssd_a_log, "ssd_d_skip": ssd_d_skip, "ssd_gate_norm_g": ssd_gate_norm_g,
        "ssd_w_mem_kv": ssd_w_mem_kv, "ssd_w_out": ssd_w_out,
        "cnv_norm_g": cnv_norm_g, "cnv_mem_norm_g": cnv_mem_norm_g, "cnv_w_in": cnv_w_in,
        "cnv_b_glu": cnv_b_glu, "cnv_dw_w": cnv_dw_w, "cnv_dw_b": cnv_dw_b,
        "cnv_ln_g": cnv_ln_g, "cnv_ln_b": cnv_ln_b, "cnv_w_mem_kv": cnv_w_mem_kv, "cnv_w_out": cnv_w_out,
        "ffn_norm_g": ffn_norm_g, "peer_w_query": peer_w_query, "peer_sub_keys": peer_sub_keys,
        "peer_u": peer_u, "peer_v": peer_v,
        "final_norm_g": final_norm_g,
    }


def reference(x, mem,
              ssd_norm_g, ssd_mem_norm_g, ssd_w_in, ssd_conv_w, ssd_conv_b, ssd_dt_bias,
              ssd_a_log, ssd_d_skip, ssd_gate_norm_g, ssd_w_mem_kv, ssd_w_out,
              cnv_norm_g, cnv_mem_norm_g, cnv_w_in, cnv_b_glu, cnv_dw_w, cnv_dw_b,
              cnv_ln_g, cnv_ln_b, cnv_w_mem_kv, cnv_w_out,
              ffn_norm_g, peer_w_query, peer_sub_keys, peer_u, peer_v,
              final_norm_g):
    h = x
    for i in range(DEPTH):
        j = i // 2
        if i % 2 == 0:
            xn = rms_norm(h, ssd_norm_g[j])
            mem_n = rms_norm(mem, ssd_mem_norm_g[j])
            h = h + ssd_mix(xn, mem_n, ssd_w_in[j], ssd_conv_w[j], ssd_conv_b[j], ssd_dt_bias[j],
                            ssd_a_log[j], ssd_d_skip[j], ssd_gate_norm_g[j], ssd_w_mem_kv[j], ssd_w_out[j])
        else:
            xn = rms_norm(h, cnv_norm_g[j])
            mem_n = rms_norm(mem, cnv_mem_norm_g[j])
            h = h + conformer_mix(xn, mem_n, cnv_w_in[j], cnv_b_glu[j], cnv_dw_w[j], cnv_dw_b[j],
                                  cnv_ln_g[j], cnv_ln_b[j], cnv_w_mem_kv[j], cnv_w_out[j])
        h = h + peer(rms_norm(h, ffn_norm_g[i]), peer_w_query[i], peer_sub_keys[i], peer_u[i], peer_v[i])
    return rms_norm(h, final_norm_g)
```

```python
import functools

import jax
import jax.numpy as jnp
from jax import lax
from jax.experimental import pallas as pl
from jax.experimental.pallas import tpu as pltpu

F32 = jnp.float32
BF16 = jnp.bfloat16
EPS = 1e-6
HIGHEST = lax.Precision.HIGHEST

D_MODEL = 1024
N_MEM = 256

SSD_D_INNER = 2048
SSD_HEAD_DIM = 64
SSD_HEADS = 32
SSD_GROUPS = 8
SSD_HEADS_PER_GROUP = SSD_HEADS // SSD_GROUPS
SSD_STATE = 128
SSD_CONV = 4
SSD_CHUNK = 128
SSD_XBC = 4096
SSD_GROUP_WIDTH = SSD_HEADS_PER_GROUP * SSD_HEAD_DIM

CONV_CH = 1024
CONV_WIDTH = 31
CONV_HALO = 32

CA_HEADS = 4
CA_HEAD_DIM = 256
CA_WIDTH = 1024

PEER_HEADS = 8
PEER_NKEYS = 128
PEER_EXPERTS = PEER_NKEYS * PEER_NKEYS
PEER_TOPK = 16
PEER_HALF = 128

LANES = 128
VMEM_LIMIT_MB = 48


def _cparams(semantics, vmem_mb=VMEM_LIMIT_MB):
    return pltpu.CompilerParams(dimension_semantics=semantics, vmem_limit_bytes=vmem_mb << 20)


def _rms(x, g):
    ms = jnp.mean(x * x, axis=-1, keepdims=True)
    return x * lax.rsqrt(ms + EPS) * g


def _sigmoid(x):
    return 1.0 / (1.0 + jnp.exp(-x))


def _norm_mm_body(x_ref, g_ref, w_ref, o_ref, xn_ref):
    @pl.when(pl.program_id(1) == 0)
    def _():
        xn_ref[...] = _rms(x_ref[...], g_ref[...]).astype(BF16)

    o_ref[...] = jnp.dot(xn_ref[...], w_ref[...], preferred_element_type=F32)


def norm_matmul(h, g, w, tm, tn):
    t, d = h.shape
    n = w.shape[1]
    return pl.pallas_call(
        _norm_mm_body,
        grid=(t // tm, n // tn),
        in_specs=[
            pl.BlockSpec((tm, d), lambda i, j: (i, 0)),
            pl.BlockSpec((1, d), lambda i, j: (0, 0)),
            pl.BlockSpec((d, tn), lambda i, j: (0, j)),
        ],
        out_specs=pl.BlockSpec((tm, tn), lambda i, j: (i, j)),
        out_shape=jax.ShapeDtypeStruct((t, n), F32),
        scratch_shapes=[pltpu.VMEM((tm, d), BF16)],
        compiler_params=_cparams(("parallel", "arbitrary")),
        name="norm_matmul",
    )(h, g.reshape(1, d), w)


def _ssd_body(xbc_ref, z_ref, dt_ref, cw_ref, cb_ref, dtb_ref, alog_ref, dfull_ref, gng_ref, exp_ref,
              o_ref, stage_ref, xc_ref, state_ref, y_ref):
    q = SSD_CHUNK
    c = pl.program_id(1)

    @pl.when(c == 0)
    def _():
        stage_ref[0:8, :] = jnp.zeros((8, SSD_XBC), F32)
        state_ref[...] = jnp.zeros_like(state_ref)

    stage_ref[8:8 + q, :] = xbc_ref[...]
    cw = 512
    for cc in range(SSD_XBC // cw):
        cs = slice(cc * cw, (cc + 1) * cw)
        acc = cb_ref[:, cs] + cw_ref[3:4, cs] * stage_ref[8:8 + q, cs]
        for j in range(1, SSD_CONV):
            acc = acc + cw_ref[3 - j:4 - j, cs] * stage_ref[8 - j:8 - j + q, cs]
        xc_ref[:, cs] = acc * _sigmoid(acc)
    stage_ref[0:8, :] = stage_ref[q:q + 8, :]

    dt_raw = dt_ref[...] + dtb_ref[...]
    dt = jnp.maximum(dt_raw, 0.0) + jnp.log1p(jnp.exp(-jnp.abs(dt_raw)))
    a = -jnp.exp(alog_ref[...])
    da = dt * a
    row = lax.broadcasted_iota(jnp.int32, (q, q), 0)
    col = lax.broadcasted_iota(jnp.int32, (q, q), 1)
    causal = row >= col
    tril = jnp.where(causal, 1.0, 0.0).astype(F32)
    acum = jnp.dot(tril, da, precision=HIGHEST, preferred_element_type=F32)
    alast = acum[q - 1:q, :]
    w_end = dt * jnp.exp(alast - acum)
    acum_t = acum.T
    dt_t = dt.T
    w_end_t = w_end.T
    cdec = jnp.dot(jnp.broadcast_to(jnp.exp(alast), (8, LANES)), exp_ref[...],
                   precision=HIGHEST, preferred_element_type=F32)[0:1, :]

    gw = SSD_GROUP_WIDTH
    lane_head = lax.broadcasted_iota(jnp.int32, (q, gw), 1) // SSD_HEAD_DIM
    for g in range(SSD_GROUPS):
        xs = xc_ref[:, g * gw:(g + 1) * gw]
        b_g = xc_ref[:, SSD_D_INNER + g * SSD_STATE:SSD_D_INNER + (g + 1) * SSD_STATE]
        c_g = xc_ref[:, SSD_D_INNER + SSD_GROUPS * SSD_STATE + g * SSD_STATE:
                     SSD_D_INNER + SSD_GROUPS * SSD_STATE + (g + 1) * SSD_STATE]
        bt_g = b_g.T
        cb = jnp.dot(c_g.astype(BF16), bt_g.astype(BF16), preferred_element_type=F32)
        prev = state_ref[g]
        xs_bf = xs.astype(BF16)
        prev_bf = prev.astype(BF16)
        zero_bf = jnp.zeros_like(xs_bf)
        m_parts, c_parts, bt_parts, xs_parts, prev_parts = [], [], [], [], []
        for r in range(SSD_HEADS_PER_GROUP):
            hd = g * SSD_HEADS_PER_GROUP + r
            a_col = jnp.broadcast_to(acum[:, hd:hd + 1], (q, q))
            decay = jnp.where(causal, jnp.exp(a_col - acum_t[hd:hd + 1, :]), 0.0)
            m_parts.append((cb * decay * dt_t[hd:hd + 1, :]).astype(BF16))
            c_parts.append((c_g * jnp.exp(a_col)).astype(BF16))
            bt_parts.append((bt_g * w_end_t[hd:hd + 1, :]).astype(BF16))
            sel = lane_head == r
            xs_parts.append(jnp.where(sel, xs_bf, zero_bf))
            prev_parts.append(jnp.where(sel, prev_bf, zero_bf))
        rhs_x = jnp.concatenate(xs_parts, axis=0)
        rhs = jnp.concatenate([rhs_x] + prev_parts, axis=0)
        lhs = jnp.concatenate(m_parts + c_parts, axis=1)
        y_g = jnp.dot(lhs, rhs, preferred_element_type=F32)
        y_ref[:, g * gw:(g + 1) * gw] = y_g + dfull_ref[:, g * gw:(g + 1) * gw] * xs
        s_new = jnp.dot(jnp.concatenate(bt_parts, axis=1), rhs_x, preferred_element_type=F32)
        state_ref[g] = prev * cdec[:, g * gw:(g + 1) * gw] + s_new

    zz = z_ref[...]
    yz = y_ref[...] * (zz * _sigmoid(zz))
    o_ref[...] = _rms(yz, gng_ref[...]).astype(BF16)


def ssd_chunk(proj, dt_raw, conv_w, conv_b, dt_bias, a_log, d_skip, gate_norm_g, batch, seq):
    nc = seq // SSD_CHUNK
    q = SSD_CHUNK
    pad = LANES - SSD_HEADS
    dtb = jnp.pad(dt_bias.astype(F32), (0, pad)).reshape(1, LANES)
    alog = jnp.pad(a_log.astype(F32), (0, pad)).reshape(1, LANES)
    dfull = jnp.repeat(d_skip.astype(F32), SSD_HEAD_DIM).reshape(1, SSD_D_INNER)
    expand = (jnp.arange(LANES)[:, None] == (jnp.arange(SSD_D_INNER)[None, :] // SSD_HEAD_DIM)).astype(F32)
    tok = lambda b, c: (b * nc + c, 0)
    const = lambda b, c: (0, 0)
    return pl.pallas_call(
        _ssd_body,
        grid=(batch, nc),
        in_specs=[
            pl.BlockSpec((q, SSD_XBC), tok),
            pl.BlockSpec((q, SSD_D_INNER), lambda b, c: (b * nc + c, SSD_XBC // SSD_D_INNER)),
            pl.BlockSpec((q, LANES), tok),
            pl.BlockSpec((SSD_CONV, SSD_XBC), const),
            pl.BlockSpec((1, SSD_XBC), const),
            pl.BlockSpec((1, LANES), const),
            pl.BlockSpec((1, LANES), const),
            pl.BlockSpec((1, SSD_D_INNER), const),
            pl.BlockSpec((1, SSD_D_INNER), const),
            pl.BlockSpec((LANES, SSD_D_INNER), const),
        ],
        out_specs=pl.BlockSpec((q, SSD_D_INNER), tok),
        out_shape=jax.ShapeDtypeStruct((batch * seq, SSD_D_INNER), BF16),
        scratch_shapes=[
            pltpu.VMEM((q + 8, SSD_XBC), F32),
            pltpu.VMEM((q, SSD_XBC), F32),
            pltpu.VMEM((SSD_GROUPS, SSD_STATE, SSD_GROUP_WIDTH), F32),
            pltpu.VMEM((q, SSD_D_INNER), F32),
        ],
        compiler_params=_cparams(("parallel", "arbitrary")),
        name="ssd_chunk",
    )(proj, proj, dt_raw, conv_w, conv_b.reshape(1, SSD_XBC), dtb, alog, dfull,
      gate_norm_g.reshape(1, SSD_D_INNER), expand)


def _mem_kv_body(mem_ref, g_ref, w_ref, kt_ref, v_ref):
    mn = _rms(mem_ref[0], g_ref[...]).astype(BF16)
    kv = jnp.dot(mn, w_ref[...], preferred_element_type=F32)
    kt_ref[0] = kv[:, :CA_WIDTH].T.astype(BF16)
    v_ref[0] = kv[:, CA_WIDTH:].astype(BF16)


def mem_kv(mem, g, w):
    b, m, d = mem.shape
    return pl.pallas_call(
        _mem_kv_body,
        grid=(b,),
        in_specs=[
            pl.BlockSpec((1, m, d), lambda i: (i, 0, 0)),
            pl.BlockSpec((1, d), lambda i: (0, 0)),
            pl.BlockSpec((d, 2 * CA_WIDTH), lambda i: (0, 0)),
        ],
        out_specs=[
            pl.BlockSpec((1, CA_WIDTH, m), lambda i: (i, 0, 0)),
            pl.BlockSpec((1, m, CA_WIDTH), lambda i: (i, 0, 0)),
        ],
        out_shape=[
            jax.ShapeDtypeStruct((b, CA_WIDTH, m), BF16),
            jax.ShapeDtypeStruct((b, m, CA_WIDTH), BF16),
        ],
        compiler_params=_cparams(("parallel",)),
        name="mem_kv",
    )(mem, g.reshape(1, d), w)


def _mix_out_body(y_ref, q_ref, kt_ref, v_ref, w1_ref, w2_ref, h_ref, g_ref, ho_ref, xnt_ref):
    outs = []
    for hd in range(CA_HEADS):
        hs = slice(hd * CA_HEAD_DIM, (hd + 1) * CA_HEAD_DIM)
        s = jnp.dot(q_ref[:, hs].astype(BF16), kt_ref[0, hs, :], preferred_element_type=F32)
        s = s * (CA_HEAD_DIM ** -0.5)
        p = jnp.exp(s - jnp.max(s, axis=-1, keepdims=True))
        p = p / jnp.sum(p, axis=-1, keepdims=True)
        outs.append(jnp.dot(p.astype(BF16), v_ref[0, :, hs], preferred_element_type=F32))
    o_mem = jnp.concatenate(outs, axis=1).astype(BF16)
    h_new = (h_ref[...]
             + jnp.dot(y_ref[...], w1_ref[...], preferred_element_type=F32)
             + jnp.dot(o_mem, w2_ref[...], preferred_element_type=F32))
    ho_ref[...] = h_new
    xnt_ref[...] = _rms(h_new, g_ref[...]).T.astype(BF16)


def mix_out(y, proj, q_block, kt, v, w1, w2, h, g_next, seq, tm):
    t, wy = y.shape
    d = h.shape[1]
    per_b = seq // tm
    return pl.pallas_call(
        _mix_out_body,
        grid=(t // tm,),
        in_specs=[
            pl.BlockSpec((tm, wy), lambda i: (i, 0)),
            pl.BlockSpec((tm, CA_WIDTH), lambda i: (i, q_block)),
            pl.BlockSpec((1, CA_WIDTH, N_MEM), lambda i: (i // per_b, 0, 0)),
            pl.BlockSpec((1, N_MEM, CA_WIDTH), lambda i: (i // per_b, 0, 0)),
            pl.BlockSpec((wy, d), lambda i: (0, 0)),
            pl.BlockSpec((CA_WIDTH, d), lambda i: (0, 0)),
            pl.BlockSpec((tm, d), lambda i: (i, 0)),
            pl.BlockSpec((1, d), lambda i: (0, 0)),
        ],
        out_specs=[
            pl.BlockSpec((tm, d), lambda i: (i, 0)),
            pl.BlockSpec((d, tm), lambda i: (0, i)),
        ],
        out_shape=[
            jax.ShapeDtypeStruct((t, d), F32),
            jax.ShapeDtypeStruct((d, t), BF16),
        ],
        compiler_params=_cparams(("parallel",)),
        name="mix_out",
    )(y, proj, kt, v, w1, w2, h, g_next.reshape(1, d))


def _conf_conv_body(a_ref, b_ref, bga_ref, bgb_ref, w_ref, wb_ref, lng_ref, lnb_ref, o_ref,
                    stage_ref, yc_ref, *, tc):
    c = pl.program_id(1)
    halo = CONV_HALO

    @pl.when(c == 0)
    def _():
        stage_ref[0:halo, :] = jnp.zeros((halo, CONV_CH), F32)

    stage_ref[halo:halo + tc, :] = (a_ref[...] + bga_ref[...]) * _sigmoid(b_ref[...] + bgb_ref[...])
    first = halo - (CONV_WIDTH - 1)
    rb, cw = 64, 512
    for cc in range(CONV_CH // cw):
        cs = slice(cc * cw, (cc + 1) * cw)
        for r0 in range(0, tc, rb):
            acc = jnp.broadcast_to(wb_ref[:, cs], (rb, cw))
            for k in range(CONV_WIDTH):
                acc = acc + w_ref[k:k + 1, cs] * stage_ref[first + k + r0:first + k + r0 + rb, cs]
            yc_ref[r0:r0 + rb, cs] = acc
    stage_ref[0:halo, :] = stage_ref[tc:tc + halo, :]

    y = yc_ref[...]
    mu = jnp.mean(y, axis=-1, keepdims=True)
    yc = y - mu
    var = jnp.mean(yc * yc, axis=-1, keepdims=True)
    yn = yc * lax.rsqrt(var + EPS) * lng_ref[...] + lnb_ref[...]
    o_ref[...] = (yn * _sigmoid(yn)).astype(BF16)


def conf_conv(proj, b_glu, dw_w, dw_b, ln_g, ln_b, batch, seq, tc):
    nc = seq // tc
    ch = CONV_CH
    const = lambda b, c: (0, 0)
    return pl.pallas_call(
        functools.partial(_conf_conv_body, tc=tc),
        grid=(batch, nc),
        in_specs=[
            pl.BlockSpec((tc, ch), lambda b, c: (b * nc + c, 0)),
            pl.BlockSpec((tc, ch), lambda b, c: (b * nc + c, 1)),
            pl.BlockSpec((1, ch), const),
            pl.BlockSpec((1, ch), const),
            pl.BlockSpec((CONV_WIDTH, ch), const),
            pl.BlockSpec((1, ch), const),
            pl.BlockSpec((1, ch), const),
            pl.BlockSpec((1, ch), const),
        ],
        out_specs=pl.BlockSpec((tc, ch), lambda b, c: (b * nc + c, 0)),
        out_shape=jax.ShapeDtypeStruct((batch * seq, ch), BF16),
        scratch_shapes=[pltpu.VMEM((tc + CONV_HALO, ch), F32), pltpu.VMEM((tc, ch), F32)],
        compiler_params=_cparams(("parallel", "arbitrary")),
        name="conf_conv",
    )(proj, proj, b_glu[:ch].reshape(1, ch), b_glu[ch:].reshape(1, ch), dw_w, dw_b.reshape(1, ch),
      ln_g.reshape(1, ch), ln_b.reshape(1, ch))


def _peer_fold_body(k1_ref, k2_ref, wq1_ref, wq2_ref, o1_ref, o2_ref):
    dims = (((1,), (1,)), ((), ()))
    o1_ref[...] = lax.dot_general(k1_ref[0], wq1_ref[...], dims, precision=HIGHEST,
                                  preferred_element_type=F32).astype(BF16)
    o2_ref[...] = lax.dot_general(k2_ref[0], wq2_ref[...], dims, precision=HIGHEST,
                                  preferred_element_type=F32).astype(BF16)


def peer_fold(sub_keys, w_query):
    d = w_query.shape[0]
    nh, nk = PEER_HEADS, PEER_NKEYS
    keys = sub_keys.reshape(nh * 2, nk, PEER_HALF)
    wf1, wf2 = pl.pallas_call(
        _peer_fold_body,
        grid=(nh,),
        in_specs=[
            pl.BlockSpec((1, nk, PEER_HALF), lambda hd: (2 * hd, 0, 0)),
            pl.BlockSpec((1, nk, PEER_HALF), lambda hd: (2 * hd + 1, 0, 0)),
            pl.BlockSpec((d, PEER_HALF), lambda hd: (0, 2 * hd)),
            pl.BlockSpec((d, PEER_HALF), lambda hd: (0, 2 * hd + 1)),
        ],
        out_specs=[
            pl.BlockSpec((nk, d), lambda hd: (0, hd)),
            pl.BlockSpec((nk, d), lambda hd: (hd, 0)),
        ],
        out_shape=[
            jax.ShapeDtypeStruct((nk, nh * d), BF16),
            jax.ShapeDtypeStruct((nh * nk, d), BF16),
        ],
        compiler_params=_cparams(("parallel",)),
        name="peer_fold",
    )(keys, keys, w_query, w_query)
    return wf1.reshape(nk * nh, d), wf2


def _compare_exchange(xs, i, l, descending):
    hi = jnp.maximum(xs[i], xs[l])
    lo = jnp.minimum(xs[i], xs[l])
    xs[i], xs[l] = (hi, lo) if descending else (lo, hi)


def _bitonic_merge_desc(xs):
    xs = list(xs)
    n = len(xs)
    j = n // 2
    while j >= 1:
        for i in range(n):
            l = i ^ j
            if l > i:
                _compare_exchange(xs, i, l, True)
        j //= 2
    return xs


def _bitonic_sort_desc(xs):
    xs = list(xs)
    n = len(xs)
    k = 2
    while k <= n:
        j = k // 2
        while j >= 1:
            for i in range(n):
                l = i ^ j
                if l > i:
                    _compare_exchange(xs, i, l, (i & k) == 0)
            j //= 2
        k *= 2
    return xs


def _merge_top(xs, ys):
    n = len(xs)
    return _bitonic_merge_desc([jnp.maximum(xs[i], ys[n - 1 - i]) for i in range(n)])


def _peer_route_body(wf1_ref, wf2_ref, xnt_ref, st_ref, stat_ref):
    nh, nk, k = PEER_HEADS, PEER_NKEYS, PEER_TOPK
    half = nh * nk
    st_ref[0:half, :] = jnp.dot(wf1_ref[...], xnt_ref[...], preferred_element_type=F32)
    st_ref[half:2 * half, :] = jnp.dot(wf2_ref[...], xnt_ref[...], preferred_element_type=F32)

    def sorted_top(slab):
        top = _bitonic_sort_desc([slab(key) for key in range(k)])
        for grp in range(1, nk // k):
            top = _merge_top(top, _bitonic_sort_desc([slab(grp * k + key) for key in range(k)]))
        return top

    a = sorted_top(lambda key: st_ref[key * nh:(key + 1) * nh, :])
    b = sorted_top(lambda key: st_ref[pl.ds(half + key, nh, stride=nk), :])
    cand = [a[i] + b[j] for i in range(k) for j in range(k) if (i + 1) * (j + 1) <= k]
    neg = jnp.full_like(cand[0], -jnp.inf)
    cs = _bitonic_sort_desc(cand + [neg] * (64 - len(cand)))
    z = jnp.ones_like(cs[0])
    for i in range(1, k):
        z = z + jnp.exp(cs[i] - cs[0])
    stat_ref[0:nh, :] = cs[k - 1]
    stat_ref[nh:2 * nh, :] = a[0]
    stat_ref[2 * nh:3 * nh, :] = b[0] + jnp.log(z)
    stat_ref[3 * nh:4 * nh, :] = jnp.zeros_like(z)


def peer_route(wf1, wf2, xnt):
    half, d = wf1.shape
    t = xnt.shape[1]
    tc = LANES
    return pl.pallas_call(
        _peer_route_body,
        grid=(t // tc,),
        in_specs=[
            pl.BlockSpec((half, d), lambda i: (0, 0)),
            pl.BlockSpec((half, d), lambda i: (0, 0)),
            pl.BlockSpec((d, tc), lambda i: (0, i)),
        ],
        out_specs=[
            pl.BlockSpec((2 * half, tc), lambda i: (0, i)),
            pl.BlockSpec((4 * PEER_HEADS, tc), lambda i: (0, i)),
        ],
        out_shape=[
            jax.ShapeDtypeStruct((2 * half, t), F32),
            jax.ShapeDtypeStruct((4 * PEER_HEADS, t), F32),
        ],
        compiler_params=_cparams(("parallel",)),
        name="peer_route",
    )(wf1, wf2, xnt)


def _gelu_tanh(x):
    return 0.5 * x * (1.0 + jnp.tanh(0.7978845608028654 * (x + 0.044715 * (x * x * x))))


def _peer_experts_body(xnt_ref, st_ref, stat_ref, u_ref, vt_ref, h_ref, g_ref, o_ref,
                       acc_ref, al_ref, be_ref, hid_ref, w_ref, *, eb, tb, final_norm):
    nh = PEER_HEADS
    nk = PEER_NKEYS
    e = pl.program_id(1)

    half = nh * nk

    @pl.when(e == 0)
    def _():
        acc_ref[...] = jnp.zeros_like(acc_ref)
        a0 = stat_ref[nh:2 * nh, :]

        def alpha_rows(key, carry):
            rs = pl.ds(pl.multiple_of(key * nh, nh), nh)
            al_ref[rs, :] = jnp.exp(st_ref[rs, :] - a0)
            return carry

        lax.fori_loop(0, nk, alpha_rows, 0)
        for hd in range(nh):
            be_ref[hd * nk:(hd + 1) * nk, :] = jnp.exp(
                st_ref[half + hd * nk:half + (hd + 1) * nk, :] - stat_ref[2 * nh + hd:2 * nh + hd + 1, :])

    hid_ref[...] = jnp.dot(u_ref[...], xnt_ref[...], preferred_element_type=F32)
    rows_per_step = eb // nk

    def row_block(j, carry):
        i1 = e * rows_per_step + j
        r0 = pl.multiple_of(j * nk, nk)
        hrows = pl.ds(pl.multiple_of(i1 * nh, nh), nh)
        for lc in range(tb // LANES):
            ls = slice(lc * LANES, (lc + 1) * LANES)
            s1_all = st_ref[hrows, ls]
            al_all = al_ref[hrows, ls]
            gate = jnp.zeros((nk, LANES), F32)
            for hd in range(nh):
                s = s1_all[hd:hd + 1, :] + st_ref[half + hd * nk:half + (hd + 1) * nk, ls]
                keep = s >= stat_ref[hd:hd + 1, ls]
                gate = gate + jnp.where(keep, be_ref[hd * nk:(hd + 1) * nk, ls], 0.0) * al_all[hd:hd + 1, :]
            w_ref[pl.ds(r0, nk), ls] = (gate * _gelu_tanh(hid_ref[pl.ds(r0, nk), ls])).astype(BF16)
        return carry

    lax.fori_loop(0, rows_per_step, row_block, 0)
    acc_ref[...] += jnp.dot(vt_ref[...], w_ref[...], preferred_element_type=F32)

    @pl.when(e == pl.num_programs(1) - 1)
    def _():
        out = h_ref[...] + acc_ref[...].T
        if final_norm:
            out = _rms(out, g_ref[...])
        o_ref[...] = out


def peer_experts(xnt, st, stat, u, vt, h, g, tb, eb, final_norm):
    d, t = xnt.shape
    rows = st.shape[0]
    ne = u.shape[0]
    body = functools.partial(_peer_experts_body, eb=eb, tb=tb, final_norm=final_norm)
    return pl.pallas_call(
        body,
        grid=(t // tb, ne // eb),
        in_specs=[
            pl.BlockSpec((d, tb), lambda i, e: (0, i)),
            pl.BlockSpec((rows, tb), lambda i, e: (0, i)),
            pl.BlockSpec((4 * PEER_HEADS, tb), lambda i, e: (0, i)),
            pl.BlockSpec((eb, d), lambda i, e: (e, 0)),
            pl.BlockSpec((d, eb), lambda i, e: (0, e)),
            pl.BlockSpec((tb, d), lambda i, e: (i, 0)),
            pl.BlockSpec((1, d), lambda i, e: (0, 0)),
        ],
        out_specs=pl.BlockSpec((tb, d), lambda i, e: (i, 0)),
        out_shape=jax.ShapeDtypeStruct((t, d), F32),
        scratch_shapes=[
            pltpu.VMEM((d, tb), F32),
            pltpu.VMEM((PEER_HEADS * PEER_NKEYS, tb), F32),
            pltpu.VMEM((PEER_HEADS * PEER_NKEYS, tb), F32),
            pltpu.VMEM((eb, tb), F32),
            pltpu.VMEM((eb, tb), BF16),
        ],
        compiler_params=_cparams(("parallel", "arbitrary")),
        name="peer_experts",
    )(xnt, st, stat, u, vt, h, g.reshape(1, d))


def peer_layer(h, xnt, w_query, sub_keys, expert_u, expert_v, g_final, final_norm, tb, eb):
    wf1, wf2 = peer_fold(sub_keys, w_query)
    st, stat = peer_route(wf1, wf2, xnt)
    return peer_experts(xnt, st, stat, expert_u.astype(BF16), expert_v.T.astype(BF16), h, g_final,
                        tb, eb, final_norm)


def _tile(n, pref):
    return pref if n % pref == 0 else n


def kernel(x, mem, ssd_norm_g, ssd_mem_norm_g, ssd_w_in, ssd_conv_w, ssd_conv_b, ssd_dt_bias, ssd_a_log, ssd_d_skip, ssd_gate_norm_g, ssd_w_mem_kv, ssd_w_out, cnv_norm_g, cnv_mem_norm_g, cnv_w_in, cnv_b_glu, cnv_dw_w, cnv_dw_b, cnv_ln_g, cnv_ln_b, cnv_w_mem_kv, cnv_w_out, ffn_norm_g, peer_w_query, peer_sub_keys, peer_u, peer_v, final_norm_g):
    batch, seq, d = x.shape
    t = batch * seq
    h = x.reshape(t, d)
    tm = _tile(t, 1024)
    tmix = _tile(seq, 512)
    tb = _tile(t, 512)
    eb = 1024

    w_in = ssd_w_in[0]
    z_end = SSD_D_INNER
    xbc_end = z_end + SSD_XBC
    dt_end = xbc_end + SSD_HEADS
    w_main = jnp.concatenate([w_in[:, z_end:xbc_end], w_in[:, :z_end], w_in[:, dt_end:]], axis=1).astype(BF16)
    w_dt = jnp.pad(w_in[:, xbc_end:dt_end], ((0, 0), (0, LANES - SSD_HEADS))).astype(BF16)
    proj = norm_matmul(h, ssd_norm_g[0], w_main, tm, 1024)
    dt_raw = norm_matmul(h, ssd_norm_g[0], w_dt, tm, LANES)
    y = ssd_chunk(proj, dt_raw, ssd_conv_w[0], ssd_conv_b[0], ssd_dt_bias[0], ssd_a_log[0], ssd_d_skip[0],
                  ssd_gate_norm_g[0], batch, seq)
    kt, v = mem_kv(mem, ssd_mem_norm_g[0], ssd_w_mem_kv[0].astype(BF16))
    w_out = ssd_w_out[0].astype(BF16)
    h, xnt = mix_out(y, proj, (SSD_XBC + SSD_D_INNER) // CA_WIDTH, kt, v, w_out[:SSD_D_INNER],
                     w_out[SSD_D_INNER:], h, ffn_norm_g[0], seq, tmix)
    h = peer_layer(h, xnt, peer_w_query[0], peer_sub_keys[0], peer_u[0], peer_v[0], final_norm_g, False,
                   tb, eb)

    proj = norm_matmul(h, cnv_norm_g[0], cnv_w_in[0].astype(BF16), tm, 1024)
    u = conf_conv(proj, cnv_b_glu[0], cnv_dw_w[0], cnv_dw_b[0], cnv_ln_g[0], cnv_ln_b[0], batch, seq,
                  _tile(seq, 256))
    kt, v = mem_kv(mem, cnv_mem_norm_g[0], cnv_w_mem_kv[0].astype(BF16))
    w_out = cnv_w_out[0].astype(BF16)
    h, xnt = mix_out(u, proj, 2 * CONV_CH // CA_WIDTH, kt, v, w_out[:CONV_CH], w_out[CONV_CH:], h,
                     ffn_norm_g[1], seq, tmix)
    out = peer_layer(h, xnt, peer_w_query[1], peer_sub_keys[1], peer_u[1], peer_v[1], final_norm_g, True,
                     tb, eb)
    return out.reshape(batch, seq, d)
```

```python
import functools

import jax
import jax.numpy as jnp
from jax import lax
from jax.experimental import pallas as pl
from jax.experimental.pallas import tpu as pltpu

F32 = jnp.float32
BF16 = jnp.bfloat16
EPS = 1e-6
HIGHEST = lax.Precision.HIGHEST

D_MODEL = 1024
N_MEM = 256

SSD_D_INNER = 2048
SSD_HEAD_DIM = 64
SSD_HEADS = 32
SSD_GROUPS = 8
SSD_HEADS_PER_GROUP = SSD_HEADS // SSD_GROUPS
SSD_STATE = 128
SSD_CONV = 4
SSD_CHUNK = 128
SSD_XBC = 4096
SSD_GROUP_WIDTH = SSD_HEADS_PER_GROUP * SSD_HEAD_DIM

CONV_CH = 1024
CONV_WIDTH = 31
CONV_HALO = 32

CA_HEADS = 4
CA_HEAD_DIM = 256
CA_WIDTH = 1024

PEER_HEADS = 8
PEER_NKEYS = 128
PEER_EXPERTS = PEER_NKEYS * PEER_NKEYS
PEER_TOPK = 16
PEER_HALF = 128

LANES = 128
VMEM_LIMIT_MB = 48


def _cparams(semantics, vmem_mb=VMEM_LIMIT_MB):
    return pltpu.CompilerParams(dimension_semantics=semantics, vmem_limit_bytes=vmem_mb << 20)


def _rms(x, g):
    ms = jnp.mean(x * x, axis=-1, keepdims=True)
    return x * lax.rsqrt(ms + EPS) * g


def _sigmoid(x):
    return 1.0 / (1.0 + jnp.exp(-x))


def _norm_mm_body(x_ref, g_ref, w_ref, o_ref, xn_ref):
    @pl.when(pl.program_id(1) == 0)
    def _():
        xn_ref[...] = _rms(x_ref[...], g_ref[...]).astype(BF16)

    o_ref[...] = jnp.dot(xn_ref[...], w_ref[...], preferred_element_type=F32)


def norm_matmul(h, g, w, tm, tn):
    t, d = h.shape
    n = w.shape[1]
    return pl.pallas_call(
        _norm_mm_body,
        grid=(t // tm, n // tn),
        in_specs=[
            pl.BlockSpec((tm, d), lambda i, j: (i, 0)),
            pl.BlockSpec((1, d), lambda i, j: (0, 0)),
            pl.BlockSpec((d, tn), lambda i, j: (0, j)),
        ],
        out_specs=pl.BlockSpec((tm, tn), lambda i, j: (i, j)),
        out_shape=jax.ShapeDtypeStruct((t, n), F32),
        scratch_shapes=[pltpu.VMEM((tm, d), BF16)],
        compiler_params=_cparams(("parallel", "arbitrary")),
        name="norm_matmul",
    )(h, g.reshape(1, d), w)


def _ssd_body(xbc_ref, z_ref, dt_ref, cw_ref, cb_ref, dtb_ref, alog_ref, dfull_ref, gng_ref, exp_ref,
              o_ref, stage_ref, xc_ref, state_ref, y_ref):
    q = SSD_CHUNK
    c = pl.program_id(1)

    @pl.when(c == 0)
    def _():
        stage_ref[0:8, :] = jnp.zeros((8, SSD_XBC), F32)
        state_ref[...] = jnp.zeros_like(state_ref)

    stage_ref[8:8 + q, :] = xbc_ref[...]
    cw = 512
    for cc in range(SSD_XBC // cw):
        cs = slice(cc * cw, (cc + 1) * cw)
        acc = cb_ref[:, cs] + cw_ref[3:4, cs] * stage_ref[8:8 + q, cs]
        for j in range(1, SSD_CONV):
            acc = acc + cw_ref[3 - j:4 - j, cs] * stage_ref[8 - j:8 - j + q, cs]
        xc_ref[:, cs] = acc * _sigmoid(acc)
    stage_ref[0:8, :] = stage_ref[q:q + 8, :]

    dt_raw = dt_ref[...] + dtb_ref[...]
    dt = jnp.maximum(dt_raw, 0.0) + jnp.log1p(jnp.exp(-jnp.abs(dt_raw)))
    a = -jnp.exp(alog_ref[...])
    da = dt * a
    row = lax.broadcasted_iota(jnp.int32, (q, q), 0)
    col = lax.broadcasted_iota(jnp.int32, (q, q), 1)
    causal = row >= col
    tril = jnp.where(causal, 1.0, 0.0).astype(F32)
    acum = jnp.dot(tril, da, precision=HIGHEST, preferred_element_type=F32)
    alast = acum[q - 1:q, :]
    w_end = dt * jnp.exp(alast - acum)
    acum_t = acum.T
    dt_t = dt.T
    w_end_t = w_end.T
    cdec = jnp.dot(jnp.broadcast_to(jnp.exp(alast), (8, LANES)), exp_ref[...],
                   precision=HIGHEST, preferred_element_type=F32)[0:1, :]

    gw = SSD_GROUP_WIDTH
    lane_head = lax.broadcasted_iota(jnp.int32, (q, gw), 1) // SSD_HEAD_DIM
    for g in range(SSD_GROUPS):
        xs = xc_ref[:, g * gw:(g + 1) * gw]
        b_g = xc_ref[:, SSD_D_INNER + g * SSD_STATE:SSD_D_INNER + (g + 1) * SSD_STATE]
        c_g = xc_ref[:, SSD_D_INNER + SSD_GROUPS * SSD_STATE + g * SSD_STATE:
                     SSD_D_INNER + SSD_GROUPS * SSD_STATE + (g + 1) * SSD_STATE]
        bt_g = b_g.T
        cb = jnp.dot(c_g.astype(BF16), bt_g.astype(BF16), preferred_element_type=F32)
        prev = state_ref[g]
        xs_bf = xs.astype(BF16)
        prev_bf = prev.astype(BF16)
        zero_bf = jnp.zeros_like(xs_bf)
        m_parts, c_parts, bt_parts, xs_parts, prev_parts = [], [], [], [], []
        for r in range(SSD_HEADS_PER_GROUP):
            hd = g * SSD_HEADS_PER_GROUP + r
            a_col = jnp.broadcast_to(acum[:, hd:hd + 1], (q, q))
            decay = jnp.where(causal, jnp.exp(a_col - acum_t[hd:hd + 1, :]), 0.0)
            m_parts.append((cb * decay * dt_t[hd:hd + 1, :]).astype(BF16))
            c_parts.append((c_g * jnp.exp(a_col)).astype(BF16))
            bt_parts.append((bt_g * w_end_t[hd:hd + 1, :]).astype(BF16))
            sel = lane_head == r
            xs_parts.append(jnp.where(sel, xs_bf, zero_bf))
            prev_parts.append(jnp.where(sel, prev_bf, zero_bf))
        rhs_x = jnp.concatenate(xs_parts, axis=0)
        rhs = jnp.concatenate([rhs_x] + prev_parts, axis=0)
        lhs = jnp.concatenate(m_parts + c_parts, axis=1)
        y_g = jnp.dot(lhs, rhs, preferred_element_type=F32)
        y_ref[:, g * gw:(g + 1) * gw] = y_g + dfull_ref[:, g * gw:(g + 1) * gw] * xs
        s_new = jnp.dot(jnp.concatenate(bt_parts, axis=1), rhs_x, preferred_element_type=F32)
        state_ref[g] = prev * cdec[:, g * gw:(g + 1) * gw] + s_new

    zz = z_ref[...]
    yz = y_ref[...] * (zz * _sigmoid(zz))
    o_ref[...] = _rms(yz, gng_ref[...]).astype(BF16)


def ssd_chunk(proj, dt_raw, conv_w, conv_b, dt_bias, a_log, d_skip, gate_norm_g, batch, seq):
    nc = seq // SSD_CHUNK
    q = SSD_CHUNK
    pad = LANES - SSD_HEADS
    dtb = jnp.pad(dt_bias.astype(F32), (0, pad)).reshape(1, LANES)
    alog = jnp.pad(a_log.astype(F32), (0, pad)).reshape(1, LANES)
    dfull = jnp.repeat(d_skip.astype(F32), SSD_HEAD_DIM).reshape(1, SSD_D_INNER)
    expand = (jnp.arange(LANES)[:, None] == (jnp.arange(SSD_D_INNER)[None, :] // SSD_HEAD_DIM)).astype(F32)
    tok = lambda b, c: (b * nc + c, 0)
    const = lambda b, c: (0, 0)
    return pl.pallas_call(
        _ssd_body,
        grid=(batch, nc),
        in_specs=[
            pl.BlockSpec((q, SSD_XBC), tok),
            pl.BlockSpec((q, SSD_D_INNER), lambda b, c: (b * nc + c, SSD_XBC // SSD_D_INNER)),
            pl.BlockSpec((q, LANES), tok),
            pl.BlockSpec((SSD_CONV, SSD_XBC), const),
            pl.BlockSpec((1, SSD_XBC), const),
            pl.BlockSpec((1, LANES), const),
            pl.BlockSpec((1, LANES), const),
            pl.BlockSpec((1, SSD_D_INNER), const),
            pl.BlockSpec((1, SSD_D_INNER), const),
            pl.BlockSpec((LANES, SSD_D_INNER), const),
        ],
        out_specs=pl.BlockSpec((q, SSD_D_INNER), tok),
        out_shape=jax.ShapeDtypeStruct((batch * seq, SSD_D_INNER), BF16),
        scratch_shapes=[
            pltpu.VMEM((q + 8, SSD_XBC), F32),
            pltpu.VMEM((q, SSD_XBC), F32),
            pltpu.VMEM((SSD_GROUPS, SSD_STATE, SSD_GROUP_WIDTH), F32),
            pltpu.VMEM((q, SSD_D_INNER), F32),
        ],
        compiler_params=_cparams(("parallel", "arbitrary")),
        name="ssd_chunk",
    )(proj, proj, dt_raw, conv_w, conv_b.reshape(1, SSD_XBC), dtb, alog, dfull,
      gate_norm_g.reshape(1, SSD_D_INNER), expand)


def _mem_kv_body(mem_ref, g_ref, w_ref, kt_ref, v_ref):
    mn = _rms(mem_ref[0], g_ref[...]).astype(BF16)
    kv = jnp.dot(mn, w_ref[...], preferred_element_type=F32)
    kt_ref[0] = kv[:, :CA_WIDTH].T.astype(BF16)
    v_ref[0] = kv[:, CA_WIDTH:].astype(BF16)


def mem_kv(mem, g, w):
    b, m, d = mem.shape
    return pl.pallas_call(
        _mem_kv_body,
        grid=(b,),
        in_specs=[
            pl.BlockSpec((1, m, d), lambda i: (i, 0, 0)),
            pl.BlockSpec((1, d), lambda i: (0, 0)),
            pl.BlockSpec((d, 2 * CA_WIDTH), lambda i: (0, 0)),
        ],
        out_specs=[
            pl.BlockSpec((1, CA_WIDTH, m), lambda i: (i, 0, 0)),
            pl.BlockSpec((1, m, CA_WIDTH), lambda i: (i, 0, 0)),
        ],
        out_shape=[
            jax.ShapeDtypeStruct((b, CA_WIDTH, m), BF16),
            jax.ShapeDtypeStruct((b, m, CA_WIDTH), BF16),
        ],
        compiler_params=_cparams(("parallel",)),
        name="mem_kv",
    )(mem, g.reshape(1, d), w)


def _mix_out_body(y_ref, q_ref, kt_ref, v_ref, w1_ref, w2_ref, h_ref, g_ref, ho_ref, xnt_ref):
    outs = []
    for hd in range(CA_HEADS):
        hs = slice(hd * CA_HEAD_DIM, (hd + 1) * CA_HEAD_DIM)
        s = jnp.dot(q_ref[:, hs].astype(BF16), kt_ref[0, hs, :], preferred_element_type=F32)
        s = s * (CA_HEAD_DIM ** -0.5)
        p = jnp.exp(s - jnp.max(s, axis=-1, keepdims=True))
        p = p / jnp.sum(p, axis=-1, keepdims=True)
        outs.append(jnp.dot(p.astype(BF16), v_ref[0, :, hs], preferred_element_type=F32))
    o_mem = jnp.concatenate(outs, axis=1).astype(BF16)
    h_new = (h_ref[...]
             + jnp.dot(y_ref[...], w1_ref[...], preferred_element_type=F32)
             + jnp.dot(o_mem, w2_ref[...], preferred_element_type=F32))
    ho_ref[...] = h_new
    xnt_ref[...] = _rms(h_new, g_ref[...]).T.astype(BF16)


def mix_out(y, proj, q_block, kt, v, w1, w2, h, g_next, seq, tm):
    t, wy = y.shape
    d = h.shape[1]
    per_b = seq // tm
    return pl.pallas_call(
        _mix_out_body,
        grid=(t // tm,),
        in_specs=[
            pl.BlockSpec((tm, wy), lambda i: (i, 0)),
            pl.BlockSpec((tm, CA_WIDTH), lambda i: (i, q_block)),
            pl.BlockSpec((1, CA_WIDTH, N_MEM), lambda i: (i // per_b, 0, 0)),
            pl.BlockSpec((1, N_MEM, CA_WIDTH), lambda i: (i // per_b, 0, 0)),
            pl.BlockSpec((wy, d), lambda i: (0, 0)),
            pl.BlockSpec((CA_WIDTH, d), lambda i: (0, 0)),
            pl.BlockSpec((tm, d), lambda i: (i, 0)),
            pl.BlockSpec((1, d), lambda i: (0, 0)),
        ],
        out_specs=[
            pl.BlockSpec((tm, d), lambda i: (i, 0)),
            pl.BlockSpec((d, tm), lambda i: (0, i)),
        ],
        out_shape=[
            jax.ShapeDtypeStruct((t, d), F32),
            jax.ShapeDtypeStruct((d, t), BF16),
        ],
        compiler_params=_cparams(("parallel",)),
        name="mix_out",
    )(y, proj, kt, v, w1, w2, h, g_next.reshape(1, d))


def _conf_conv_body(a_ref, b_ref, bga_ref, bgb_ref, w_ref, wb_ref, lng_ref, lnb_ref, o_ref,
                    stage_ref, yc_ref, *, tc):
    c = pl.program_id(1)
    halo = CONV_HALO

    @pl.when(c == 0)
    def _():
        stage_ref[0:halo, :] = jnp.zeros((halo, CONV_CH), F32)

    stage_ref[halo:halo + tc, :] = (a_ref[...] + bga_ref[...]) * _sigmoid(b_ref[...] + bgb_ref[...])
    first = halo - (CONV_WIDTH - 1)
    rb, cw = 64, 512
    for cc in range(CONV_CH // cw):
        cs = slice(cc * cw, (cc + 1) * cw)
        for r0 in range(0, tc, rb):
            acc = jnp.broadcast_to(wb_ref[:, cs], (rb, cw))
            for k in range(CONV_WIDTH):
                acc = acc + w_ref[k:k + 1, cs] * stage_ref[first + k + r0:first + k + r0 + rb, cs]
            yc_ref[r0:r0 + rb, cs] = acc
    stage_ref[0:halo, :] = stage_ref[tc:tc + halo, :]

    y = yc_ref[...]
    mu = jnp.mean(y, axis=-1, keepdims=True)
    yc = y - mu
    var = jnp.mean(yc * yc, axis=-1, keepdims=True)
    yn = yc * lax.rsqrt(var + EPS) * lng_ref[...] + lnb_ref[...]
    o_ref[...] = (yn * _sigmoid(yn)).astype(BF16)


def conf_conv(proj, b_glu, dw_w, dw_b, ln_g, ln_b, batch, seq, tc):
    nc = seq // tc
    ch = CONV_CH
    const = lambda b, c: (0, 0)
    return pl.pallas_call(
        functools.partial(_conf_conv_body, tc=tc),
        grid=(batch, nc),
        in_specs=[
            pl.BlockSpec((tc, ch), lambda b, c: (b * nc + c, 0)),
            pl.BlockSpec((tc, ch), lambda b, c: (b * nc + c, 1)),
            pl.BlockSpec((1, ch), const),
            pl.BlockSpec((1, ch), const),
            pl.BlockSpec((CONV_WIDTH, ch), const),
            pl.BlockSpec((1, ch), const),
            pl.BlockSpec((1, ch), const),
            pl.BlockSpec((1, ch), const),
        ],
        out_specs=pl.BlockSpec((tc, ch), lambda b, c: (b * nc + c, 0)),
        out_shape=jax.ShapeDtypeStruct((batch * seq, ch), BF16),
        scratch_shapes=[pltpu.VMEM((tc + CONV_HALO, ch), F32), pltpu.VMEM((tc, ch), F32)],
        compiler_params=_cparams(("parallel", "arbitrary")),
        name="conf_conv",
    )(proj, proj, b_glu[:ch].reshape(1, ch), b_glu[ch:].reshape(1, ch), dw_w, dw_b.reshape(1, ch),
      ln_g.reshape(1, ch), ln_b.reshape(1, ch))


def _peer_fold_body(k1_ref, k2_ref, wq1_ref, wq2_ref, o1_ref, o2_ref):
    dims = (((1,), (1,)), ((), ()))
    o1_ref[...] = lax.dot_general(k1_ref[0], wq1_ref[...], dims, precision=HIGHEST,
                                  preferred_element_type=F32).astype(BF16)
    o2_ref[...] = lax.dot_general(k2_ref[0], wq2_ref[...], dims, precision=HIGHEST,
                                  preferred_element_type=F32).astype(BF16)


def peer_fold(sub_keys, w_query):
    d = w_query.shape[0]
    nh, nk = PEER_HEADS, PEER_NKEYS
    keys = sub_keys.reshape(nh * 2, nk, PEER_HALF)
    wf1, wf2 = pl.pallas_call(
        _peer_fold_body,
        grid=(nh,),
        in_specs=[
            pl.BlockSpec((1, nk, PEER_HALF), lambda hd: (2 * hd, 0, 0)),
            pl.BlockSpec((1, nk, PEER_HALF), lambda hd: (2 * hd + 1, 0, 0)),
            pl.BlockSpec((d, PEER_HALF), lambda hd: (0, 2 * hd)),
            pl.BlockSpec((d, PEER_HALF), lambda hd: (0, 2 * hd + 1)),
        ],
        out_specs=[
            pl.BlockSpec((nk, d), lambda hd: (0, hd)),
            pl.BlockSpec((nk, d), lambda hd: (hd, 0)),
        ],
        out_shape=[
            jax.ShapeDtypeStruct((nk, nh * d), BF16),
            jax.ShapeDtypeStruct((nh * nk, d), BF16),
        ],
        compiler_params=_cparams(("parallel",)),
        name="peer_fold",
    )(keys, keys, w_query, w_query)
    return wf1.reshape(nk * nh, d), wf2


def _compare_exchange(xs, i, l, descending):
    hi = jnp.maximum(xs[i], xs[l])
    lo = jnp.minimum(xs[i], xs[l])
    xs[i], xs[l] = (hi, lo) if descending else (lo, hi)


def _bitonic_merge_desc(xs):
    xs = list(xs)
    n = len(xs)
    j = n // 2
    while j >= 1:
        for i in range(n):
            l = i ^ j
            if l > i:
                _compare_exchange(xs, i, l, True)
        j //= 2
    return xs


def _bitonic_sort_desc(xs):
    xs = list(xs)
    n = len(xs)
    k = 2
    while k <= n:
        j = k // 2
        while j >= 1:
            for i in range(n):
                l = i ^ j
                if l > i:
                    _compare_exchange(xs, i, l, (i & k) == 0)
            j //= 2
        k *= 2
    return xs


def _merge_top(xs, ys):
    n = len(xs)
    return _bitonic_merge_desc([jnp.maximum(xs[i], ys[n - 1 - i]) for i in range(n)])


PEER_GROUP = 16


def _pack_rows(x):
    return pltpu.bitcast(x.astype(BF16), jnp.uint32)


def _unpack_rows(x):
    return pltpu.bitcast(x, BF16)


def _peer_route_body(wf1_ref, wf2_ref, xnt_ref, ac_ref, r2_ref, be_ref, s1_ref, s2_ref):
    nh, nk, k = PEER_HEADS, PEER_NKEYS, PEER_TOPK
    s1_ref[...] = jnp.dot(wf1_ref[...], xnt_ref[...], preferred_element_type=F32)
    s2_ref[...] = jnp.dot(wf2_ref[...], xnt_ref[...], preferred_element_type=F32)

    def sorted_top(slab):
        top = _bitonic_sort_desc([slab(key) for key in range(k)])
        for grp in range(1, nk // k):
            top = _merge_top(top, _bitonic_sort_desc([slab(grp * k + key) for key in range(k)]))
        return top

    a = sorted_top(lambda key: s1_ref[key * nh:(key + 1) * nh, :])
    b = sorted_top(lambda key: s2_ref[pl.ds(key, nh, stride=nk), :])
    cand = [a[i] + b[j] for i in range(k) for j in range(k) if (i + 1) * (j + 1) <= k]
    neg = jnp.full_like(cand[0], -jnp.inf)
    cs = _bitonic_sort_desc(cand + [neg] * (64 - len(cand)))
    tau = cs[k - 1]
    z = jnp.ones_like(cs[0])
    for i in range(1, k):
        z = z + jnp.exp(cs[i] - cs[0])
    a0 = a[0]
    bz = b[0] + jnp.log(z)

    def key_rows(key, carry):
        s1 = s1_ref[pl.ds(pl.multiple_of(key * nh, nh), nh), :]
        cnt = jnp.zeros_like(s1)
        for q in range(k):
            cnt = cnt + jnp.where(s1 + b[q] >= tau, 1.0, 0.0)
        base = pl.multiple_of(key * 2 * nh, 2 * nh)
        ac_ref[pl.ds(base, nh), :] = cnt
        ac_ref[pl.ds(pl.multiple_of(base + nh, nh), nh), :] = jnp.exp(s1 - a0)
        return carry

    lax.fori_loop(0, nk, key_rows, 0)

    for hd in range(nh):
        bq = [b[q][hd:hd + 1, :] for q in range(k)]
        bz_h = bz[hd:hd + 1, :]
        for v in range(nk // PEER_GROUP):
            x = s2_ref[hd * nk + v * PEER_GROUP:hd * nk + (v + 1) * PEER_GROUP, :]
            rank = jnp.zeros_like(x)
            for q in range(k):
                rank = rank + jnp.where(bq[q] > x, 1.0, 0.0)
            rows = slice((hd * nk + v * PEER_GROUP) // 2, (hd * nk + (v + 1) * PEER_GROUP) // 2)
            r2_ref[rows, :] = _pack_rows(rank)
            be_ref[rows, :] = _pack_rows(jnp.exp(x - bz_h))


def peer_route(wf1, wf2, xnt):
    half, d = wf1.shape
    t = xnt.shape[1]
    tc = LANES
    return pl.pallas_call(
        _peer_route_body,
        grid=(t // tc,),
        in_specs=[
            pl.BlockSpec((half, d), lambda i: (0, 0)),
            pl.BlockSpec((half, d), lambda i: (0, 0)),
            pl.BlockSpec((d, tc), lambda i: (0, i)),
        ],
        out_specs=[
            pl.BlockSpec((2 * half, tc), lambda i: (0, i)),
            pl.BlockSpec((half // 2, tc), lambda i: (0, i)),
            pl.BlockSpec((half // 2, tc), lambda i: (0, i)),
        ],
        out_shape=[
            jax.ShapeDtypeStruct((2 * half, t), F32),
            jax.ShapeDtypeStruct((half // 2, t), jnp.uint32),
            jax.ShapeDtypeStruct((half // 2, t), jnp.uint32),
        ],
        scratch_shapes=[pltpu.VMEM((half, tc), F32), pltpu.VMEM((half, tc), F32)],
        compiler_params=_cparams(("parallel",)),
        name="peer_route",
    )(wf1, wf2, xnt)


def _gelu_tanh(x):
    c = 0.7978845608028654
    half_x = 0.5 * x
    return half_x + half_x * jnp.tanh(x * (c + (0.044715 * c) * (x * x)))


def _peer_experts_body(xnt_ref, ac_ref, r2_ref, be_ref, u_ref, vt_ref, h_ref, g_ref, o_ref,
                       acc_ref, hid_ref, w_ref, *, eb, tb, final_norm):
    nh, nk, grp = PEER_HEADS, PEER_NKEYS, PEER_GROUP
    ngrp = nk // grp
    e = pl.program_id(1)

    @pl.when(e == 0)
    def _():
        acc_ref[...] = jnp.zeros_like(acc_ref)

    hid_ref[...] = jnp.dot(u_ref[...], xnt_ref[...], preferred_element_type=F32)
    rows_per_step = eb // nk
    zero = jnp.zeros((grp, LANES), BF16)
    for j in range(rows_per_step):
        i1 = e * rows_per_step + j
        ac_rows = pl.ds(pl.multiple_of(i1 * 2 * nh, 2 * nh), 2 * nh)
        for lc in range(tb // LANES):
            ls = slice(lc * LANES, (lc + 1) * LANES)
            ac = ac_ref[ac_rows, ls]
            gates = [None] * ngrp
            for hd in range(nh):
                cnt = jnp.broadcast_to(ac[hd:hd + 1, :], (grp, LANES)).astype(BF16)
                fac = jnp.broadcast_to(ac[nh + hd:nh + hd + 1, :], (grp, LANES)).astype(BF16)
                for v in range(ngrp):
                    ps = slice((hd * nk + v * grp) // 2, (hd * nk + (v + 1) * grp) // 2)
                    keep = _unpack_rows(r2_ref[ps, ls]) < cnt
                    term = jnp.where(keep, _unpack_rows(be_ref[ps, ls]), zero) * fac
                    gates[v] = term if gates[v] is None else gates[v] + term
            for v in range(ngrp):
                rs = slice(j * nk + v * grp, j * nk + (v + 1) * grp)
                ps = slice((j * nk + v * grp) // 2, (j * nk + (v + 1) * grp) // 2)
                w_ref[ps, ls] = pltpu.bitcast(gates[v] * _gelu_tanh(hid_ref[rs, ls]).astype(BF16), jnp.uint32)
    acc_ref[...] += jnp.dot(vt_ref[...], _unpack_rows(w_ref[...]), preferred_element_type=F32)

    @pl.when(e == pl.num_programs(1) - 1)
    def _():
        out = h_ref[...] + acc_ref[...].T
        if final_norm:
            out = _rms(out, g_ref[...])
        o_ref[...] = out


def peer_experts(xnt, ac, r2, be, u, vt, h, g, tb, eb, final_norm):
    d, t = xnt.shape
    body = functools.partial(_peer_experts_body, eb=eb, tb=tb, final_norm=final_norm)
    return pl.pallas_call(
        body,
        grid=(t // tb, u.shape[0] // eb),
        in_specs=[
            pl.BlockSpec((d, tb), lambda i, e: (0, i)),
            pl.BlockSpec((ac.shape[0], tb), lambda i, e: (0, i)),
            pl.BlockSpec((r2.shape[0], tb), lambda i, e: (0, i)),
            pl.BlockSpec((be.shape[0], tb), lambda i, e: (0, i)),
            pl.BlockSpec((eb, d), lambda i, e: (e, 0)),
            pl.BlockSpec((d, eb), lambda i, e: (0, e)),
            pl.BlockSpec((tb, d), lambda i, e: (i, 0)),
            pl.BlockSpec((1, d), lambda i, e: (0, 0)),
        ],
        out_specs=pl.BlockSpec((tb, d), lambda i, e: (i, 0)),
        out_shape=jax.ShapeDtypeStruct((t, d), F32),
        scratch_shapes=[
            pltpu.VMEM((d, tb), F32),
            pltpu.VMEM((eb, tb), F32),
            pltpu.VMEM((eb // 2, tb), jnp.uint32),
        ],
        compiler_params=_cparams(("parallel", "arbitrary")),
        name="peer_experts",
    )(xnt, ac, r2, be, u, vt, h, g.reshape(1, d))


def peer_layer(h, xnt, w_query, sub_keys, expert_u, expert_v, g_final, final_norm, tb, eb):
    wf1, wf2 = peer_fold(sub_keys, w_query)
    ac, r2, be = peer_route(wf1, wf2, xnt)
    return peer_experts(xnt, ac, r2, be, expert_u.astype(BF16), expert_v.T.astype(BF16), h, g_final,
                        tb, eb, final_norm)


def _tile(n, pref):
    return pref if n % pref == 0 else n


def kernel(x, mem, ssd_norm_g, ssd_mem_norm_g, ssd_w_in, ssd_conv_w, ssd_conv_b, ssd_dt_bias, ssd_a_log, ssd_d_skip, ssd_gate_norm_g, ssd_w_mem_kv, ssd_w_out, cnv_norm_g, cnv_mem_norm_g, cnv_w_in, cnv_b_glu, cnv_dw_w, cnv_dw_b, cnv_ln_g, cnv_ln_b, cnv_w_mem_kv, cnv_w_out, ffn_norm_g, peer_w_query, peer_sub_keys, peer_u, peer_v, final_norm_g):
    batch, seq, d = x.shape
    t = batch * seq
    h = x.reshape(t, d)
    tm = _tile(t, 1024)
    tmix = _tile(seq, 512)
    tb = _tile(t, 512)
    eb = 1024

    w_in = ssd_w_in[0]
    z_end = SSD_D_INNER
    xbc_end = z_end + SSD_XBC
    dt_end = xbc_end + SSD_HEADS
    w_main = jnp.concatenate([w_in[:, z_end:xbc_end], w_in[:, :z_end], w_in[:, dt_end:]], axis=1).astype(BF16)
    w_dt = jnp.pad(w_in[:, xbc_end:dt_end], ((0, 0), (0, LANES - SSD_HEADS))).astype(BF16)
    proj = norm_matmul(h, ssd_norm_g[0], w_main, tm, 1024)
    dt_raw = norm_matmul(h, ssd_norm_g[0], w_dt, tm, LANES)
    y = ssd_chunk(proj, dt_raw, ssd_conv_w[0], ssd_conv_b[0], ssd_dt_bias[0], ssd_a_log[0], ssd_d_skip[0],
                  ssd_gate_norm_g[0], batch, seq)
    kt, v = mem_kv(mem, ssd_mem_norm_g[0], ssd_w_mem_kv[0].astype(BF16))
    w_out = ssd_w_out[0].astype(BF16)
    h, xnt = mix_out(y, proj, (SSD_XBC + SSD_D_INNER) // CA_WIDTH, kt, v, w_out[:SSD_D_INNER],
                     w_out[SSD_D_INNER:], h, ffn_norm_g[0], seq, tmix)
    h = peer_layer(h, xnt, peer_w_query[0], peer_sub_keys[0], peer_u[0], peer_v[0], final_norm_g, False,
                   tb, eb)

    proj = norm_matmul(h, cnv_norm_g[0], cnv_w_in[0].astype(BF16), tm, 1024)
    u = conf_conv(proj, cnv_b_glu[0], cnv_dw_w[0], cnv_dw_b[0], cnv_ln_g[0], cnv_ln_b[0], batch, seq,
                  _tile(seq, 256))
    kt, v = mem_kv(mem, cnv_mem_norm_g[0], cnv_w_mem_kv[0].astype(BF16))
    w_out = cnv_w_out[0].astype(BF16)
    h, xnt = mix_out(u, proj, 2 * CONV_CH // CA_WIDTH, kt, v, w_out[:CONV_CH], w_out[CONV_CH:], h,
                     ffn_norm_g[1], seq, tmix)
    out = peer_layer(h, xnt, peer_w_query[1], peer_sub_keys[1], peer_u[1], peer_v[1], final_norm_g, True,
                     tb, eb)
    return out.reshape(batch, seq, d)
```
